```python
import math
import jax, jax.numpy as jnp
from jax import lax
import numpy as np

D_MODEL = 2048
BATCH = 2
SEQ = 4096
DEPTH = 4
DEC_BATCH = 8
DEC_SEQ = 8
PAST_LEN = 16384
PAGE_SIZE = 128

MIX_WIDTH = D_MODEL
SSM_WIDTH = D_MODEL // 4
SSM_GROUP = 16
SSM_GROUPS = SSM_WIDTH // SSM_GROUP
SSM_STATE = 64
ATT_WIDTH = D_MODEL // 2
ATT_HEADS = 8
ATT_VDIM = ATT_WIDTH // ATT_HEADS
ATT_QKDIM = ATT_VDIM // 2
DN_WIDTH = MIX_WIDTH - SSM_WIDTH - ATT_WIDTH
DN_HEADS = 4
DN_HEAD_DIM = DN_WIDTH // DN_HEADS
DN_CONV = 4
DN_CHUNK = 64
FFN_DIM = 5632
PLE_DIM = 256
Q_BLOCK = 128
NORM_EPS = 1e-6

SPLIT_SIZES = (SSM_WIDTH, ATT_WIDTH, ATT_WIDTH, ATT_WIDTH, 3 * DN_WIDTH, DN_WIDTH, DN_HEADS, DN_HEADS)
IN_COLS = sum(SPLIT_SIZES)
SPLIT_IDX = tuple(int(v) for v in np.cumsum(SPLIT_SIZES)[:-1])

kernel_name = 'hymba_style_s5_diffattn_gdn_macaron_step'


def rmsnorm(x, g):
    xf = x.astype(jnp.float32)
    y = xf * lax.rsqrt(jnp.mean(xf * xf, axis=-1, keepdims=True) + NORM_EPS)
    return (y * g.astype(jnp.float32)).astype(x.dtype)


def l2norm(x):
    return x * lax.rsqrt(jnp.sum(x * x, axis=-1, keepdims=True) + NORM_EPS)


def swiglu(x, w_gate, w_up, w_down):
    return (jax.nn.silu(x @ w_gate) * (x @ w_up)) @ w_down


def _ssm_combine(left, right):
    a1, b1 = left
    a2, b2 = right
    return a1 * a2, a2 * b1 + b2


def s5_mixer(u, s0, lam_re, lam_im, b_re, b_im, c_re, c_im, d, log_dt):
    f32 = jnp.float32
    uf = u.astype(f32)
    lam = lax.complex(lam_re.astype(f32), lam_im.astype(f32))
    dt = jnp.exp(log_dt.astype(f32))[:, None]
    lam_bar = jnp.exp(lam * dt)
    b_bar = ((lam_bar - 1.0) / lam)[..., None] * lax.complex(b_re.astype(f32), b_im.astype(f32))
    c_mat = lax.complex(c_re.astype(f32), c_im.astype(f32))
    bu = jnp.einsum('btgp,gnp->btgn', uf.astype(jnp.complex64), b_bar)
    bu = bu.at[:, 0].add(lam_bar * s0)
    a = jnp.broadcast_to(lam_bar, bu.shape)
    _, s = lax.associative_scan(_ssm_combine, (a, bu), axis=1)
    y = jnp.real(jnp.einsum('btgn,gpn->btgp', s, c_mat)) + d.astype(f32) * uf
    return y, s[:, -1]


def diff_attend(q, q_pos, k, v, k_pos, lam):
    s = jnp.einsum('bqhcd,bkhcd->bhcqk', q, k).astype(jnp.float32) * (ATT_QKDIM ** -0.5)
    dist = (q_pos[:, None] - k_pos[None, :]).astype(jnp.float32)
    slopes = jnp.exp2(-8.0 * jnp.arange(1, ATT_HEADS + 1, dtype=jnp.float32) / ATT_HEADS)
    s = s - slopes[:, None, None, None] * dist
    s = jnp.where(dist >= 0, s, -jnp.inf)
    p = jax.nn.softmax(s, axis=-1)
    w = p[:, :, 0] - lam * p[:, :, 1]
    return jnp.einsum('bhqk,bkhe->bqhe', w.astype(v.dtype), v)


def diff_attention(q, k, v, q_pos, k_pos, lam):
    b, t = q.shape[:2]
    blk = Q_BLOCK if t % Q_BLOCK == 0 else t
    nb = t // blk
    qb = jnp.moveaxis(q.reshape(b, nb, blk, ATT_HEADS, 2, ATT_QKDIM), 1, 0)
    pb = q_pos.reshape(nb, blk)
    o = lax.map(lambda qp: diff_attend(qp[0], qp[1], k, v, k_pos, lam), (qb, pb))
    return jnp.moveaxis(o, 0, 1).reshape(b, t, ATT_HEADS, ATT_VDIM)


def causal_dwconv(x, buf, w):
    xx = jnp.concatenate([buf.astype(x.dtype), x], axis=1)
    y = lax.conv_general_dilated(xx, w[:, None, :].astype(x.dtype), window_strides=(1,), padding='VALID',
                                 dimension_numbers=('NWC', 'WIO', 'NWC'), feature_group_count=x.shape[-1])
    return y, xx[:, -(DN_CONV - 1):]


def gated_delta_rule(q, k, v, g, beta, s0):
    b, t, h, dk = q.shape
    dv = v.shape[-1]
    c = DN_CHUNK if t % DN_CHUNK == 0 else t
    n = t // c
    blk = lambda a: jnp.swapaxes(a.reshape(b, n, c, h, -1), 2, 3)
    q, k, v = blk(q), blk(k), blk(v)
    g = jnp.cumsum(blk(g)[..., 0], axis=-1)
    beta = blk(beta)
    causal = jnp.tril(jnp.ones((c, c), dtype=bool))
    decay = jnp.exp(jnp.where(causal, g[..., :, None] - g[..., None, :], -jnp.inf))
    kb = k * beta
    m = jnp.eye(c, dtype=jnp.float32) + jnp.tril(jnp.einsum('bnhid,bnhjd->bnhij', kb, k) * decay, -1)
    rhs = jnp.concatenate([v * beta, kb * jnp.exp(g)[..., None]], axis=-1)
    sol = lax.linalg.triangular_solve(m, rhs, left_side=True, lower=True, unit_diagonal=True)
    u, w = sol[..., :dv], sol[..., dv:]
    attn = jnp.einsum('bnhid,bnhjd->bnhij', q, k) * decay
    qg = q * jnp.exp(g)[..., None]
    kd = k * jnp.exp(g[..., -1:] - g)[..., None]
    g_last = jnp.exp(g[..., -1])

    def step(s, xs):
        u_c, w_c, a_c, qg_c, kd_c, gl_c = xs
        v_new = u_c - jnp.einsum('bhcd,bhde->bhce', w_c, s)
        o = jnp.einsum('bhcd,bhde->bhce', qg_c, s) + jnp.einsum('bhij,bhje->bhie', a_c, v_new)
        s = s * gl_c[..., None, None] + jnp.einsum('bhcd,bhce->bhde', kd_c, v_new)
        return s, o

    xs = tuple(jnp.moveaxis(a, 1, 0) for a in (u, w, attn, qg, kd, g_last))
    s_final, o = lax.scan(step, s0, xs)
    o = jnp.transpose(o, (1, 0, 3, 2, 4)).reshape(b, t, h, dv)
    return o, s_final


def layer_forward(x, ple, i, pos0, ssm0, conv0, dn0, k_past, v_past, w):
    f32 = jnp.float32
    b, t, _ = x.shape
    h = x + 0.5 * swiglu(rmsnorm(x, w['norm_ffn1'][i]), w['ffn1_w_gate'][i], w['ffn1_w_up'][i], w['ffn1_w_down'][i])
    proj = rmsnorm(h, w['norm_mix'][i]) @ w['w_in'][i]
    ssm_u, att_q, att_k, att_v, dn_qkv, dn_z, dn_a, dn_b = jnp.split(proj, SPLIT_IDX, axis=-1)

    y_ssm, ssm_new = s5_mixer(ssm_u.reshape(b, t, SSM_GROUPS, SSM_GROUP), ssm0,
                              w['ssm_lambda_re'][i], w['ssm_lambda_im'][i], w['ssm_b_re'][i], w['ssm_b_im'][i],
                              w['ssm_c_re'][i], w['ssm_c_im'][i], w['ssm_d'][i], w['ssm_log_dt'][i])
    y_ssm = jax.nn.gelu(y_ssm.reshape(b, t, SSM_WIDTH)).astype(x.dtype)
    y_ssm = y_ssm * jax.nn.sigmoid(y_ssm @ w['ssm_glu_w'][i] + w['ssm_glu_b'][i])
    out_ssm = rmsnorm(y_ssm, w['ssm_out_norm'][i])

    lam_init = 0.8 - 0.6 * math.exp(-0.3 * i)
    lam = (jnp.exp(jnp.sum(w['att_lambda_q1'][i].astype(f32) * w['att_lambda_k1'][i].astype(f32)))
           - jnp.exp(jnp.sum(w['att_lambda_q2'][i].astype(f32) * w['att_lambda_k2'][i].astype(f32))) + lam_init)
    q = att_q.reshape(b, t, ATT_HEADS, 2, ATT_QKDIM)
    k_new = att_k.reshape(b, t, ATT_HEADS, 2 * ATT_QKDIM)
    v_new = att_v.reshape(b, t, ATT_HEADS, ATT_VDIM)
    if k_past is None:
        k_all, v_all = k_new, v_new
    else:
        k_all = jnp.concatenate([k_past.astype(x.dtype), k_new], axis=1)
        v_all = jnp.concatenate([v_past.astype(x.dtype), v_new], axis=1)
    k_pos = jnp.arange(k_all.shape[1], dtype=jnp.int32)
    q_pos = pos0 + jnp.arange(t, dtype=jnp.int32)
    o_att = diff_attention(q, k_all.reshape(b, -1, ATT_HEADS, 2, ATT_QKDIM), v_all, q_pos, k_pos, lam)
    o_att = rmsnorm(o_att, w['att_subln'][i]) * (1.0 - lam_init)
    out_att = o_att.reshape(b, t, ATT_WIDTH)

    qkv, conv_new = causal_dwconv(dn_qkv, conv0, w['dn_conv_w'][i])
    dq, dk, dvv = jnp.split(jax.nn.silu(qkv).astype(f32), 3, axis=-1)
    dq = l2norm(dq.reshape(b, t, DN_HEADS, DN_HEAD_DIM)) * (DN_HEAD_DIM ** -0.5)
    dk = l2norm(dk.reshape(b, t, DN_HEADS, DN_HEAD_DIM))
    dvv = dvv.reshape(b, t, DN_HEADS, DN_HEAD_DIM)
    beta = jax.nn.sigmoid(dn_b.astype(f32))
    g = -jnp.exp(w['dn_a_log'][i].astype(f32)) * jax.nn.softplus(dn_a.astype(f32) + w['dn_dt_bias'][i].astype(f32))
    o_dn, dn_new = gated_delta_rule(dq, dk, dvv, g, beta, dn0)
    o_dn = rmsnorm(o_dn, w['dn_out_norm'][i]) * jax.nn.silu(dn_z.reshape(b, t, DN_HEADS, DN_HEAD_DIM).astype(f32))
    out_dn = o_dn.reshape(b, t, DN_WIDTH).astype(x.dtype)

    h = h + jnp.concatenate([out_ssm, out_att, out_dn], axis=-1) @ w['w_out'][i]
    h = h + 0.5 * swiglu(rmsnorm(h, w['norm_ffn2'][i]), w['ffn2_w_gate'][i], w['ffn2_w_up'][i], w['ffn2_w_down'][i])
    h = h + jax.nn.sigmoid(rmsnorm(h, w['ple_gate_norm'][i]) @ w['ple_gate_w'][i]) * (ple @ w['ple_w'][i])
    return h, (k_new, v_new, jnp.real(ssm_new), jnp.imag(ssm_new), conv_new, dn_new)


def trunk(x, p, pos0, past, w):
    b = x.shape[0]
    outs = []
    for i in range(DEPTH):
        if past is None:
            ssm0 = jnp.zeros((b, SSM_GROUPS, SSM_STATE), jnp.complex64)
            conv0 = jnp.zeros((b, DN_CONV - 1, 3 * DN_WIDTH), x.dtype)
            dn0 = jnp.zeros((b, DN_HEADS, DN_HEAD_DIM, DN_HEAD_DIM), jnp.float32)
            k_past = v_past = None
        else:
            cache_k, cache_v, page_table, s_re, s_im, s_conv, s_dn = past
            k_past = cache_k[i, page_table].reshape(b, -1, ATT_HEADS, 2 * ATT_QKDIM)
            v_past = cache_v[i, page_table].reshape(b, -1, ATT_HEADS, ATT_VDIM)
            ssm0 = lax.complex(s_re[i].astype(jnp.float32), s_im[i].astype(jnp.float32))
            conv0 = s_conv[i]
            dn0 = s_dn[i].astype(jnp.float32)
        x, new = layer_forward(x, p[i], i, pos0, ssm0, conv0, dn0, k_past, v_past, w)
        outs.append(new)
    y = rmsnorm(x, w['final_norm'])
    return y, [jnp.stack(a) for a in zip(*outs)]


def setup_inputs(seed: int = 0) -> dict:
    key = jax.random.key(seed)
    ks = iter(jax.random.split(key, 64))
    f32 = jnp.float32
    nrm = lambda shape, scale: jax.random.normal(next(ks), shape, f32) * scale
    gain = lambda shape: 1.0 + nrm(shape, 0.01)
    n_pages = PAST_LEN // PAGE_SIZE
    n_used = DEC_BATCH * n_pages
    n_phys = n_used + n_used // 4
    page_table = jax.random.permutation(next(ks), n_phys)[:n_used].reshape(DEC_BATCH, n_pages).astype(jnp.int32)
    L, D, G, N, P = DEPTH, D_MODEL, SSM_GROUPS, SSM_STATE, SSM_GROUP
    dt = jnp.exp(jax.random.uniform(next(ks), (L, DN_HEADS), f32, math.log(1e-3), math.log(1e-1)))
    return {
        'x_prompt': nrm((BATCH, SEQ, D), 1.0),
        'x_sample': nrm((DEC_BATCH, DEC_SEQ, D), 1.0),
        'cache_k': nrm((L, n_phys, PAGE_SIZE, ATT_HEADS, 2 * ATT_QKDIM), 1.0),
        'cache_v': nrm((L, n_phys, PAGE_SIZE, ATT_HEADS, ATT_VDIM), 1.0),
        'state_ssm_re': nrm((L, DEC_BATCH, G, N), 0.1),
        'state_ssm_im': nrm((L, DEC_BATCH, G, N), 0.1),
        'state_conv': nrm((L, DEC_BATCH, DN_CONV - 1, 3 * DN_WIDTH), 1.0),
        'state_delta': nrm((L, DEC_BATCH, DN_HEADS, DN_HEAD_DIM, DN_HEAD_DIM), 0.1),
        'page_table': page_table,
        'p_prompt': nrm((L, BATCH, SEQ, PLE_DIM), 1.0),
        'p_sample': nrm((L, DEC_BATCH, DEC_SEQ, PLE_DIM), 1.0),
        'norm_ffn1': gain((L, D)),
        'ffn1_w_gate': nrm((L, D, FFN_DIM), D ** -0.5),
        'ffn1_w_up': nrm((L, D, FFN_DIM), D ** -0.5),
        'ffn1_w_down': nrm((L, FFN_DIM, D), FFN_DIM ** -0.5),
        'norm_mix': gain((L, D)),
        'w_in': nrm((L, D, IN_COLS), D ** -0.5),
        'ssm_lambda_re': -0.5 + nrm((L, G, N), 0.01),
        'ssm_lambda_im': jnp.pi * jnp.arange(N, dtype=f32) + nrm((L, G, N), 0.01),
        'ssm_b_re': nrm((L, G, N, P), (2 * P) ** -0.5),
        'ssm_b_im': nrm((L, G, N, P), (2 * P) ** -0.5),
        'ssm_c_re': nrm((L, G, P, N), (2 * N) ** -0.5),
        'ssm_c_im': nrm((L, G, P, N), (2 * N) ** -0.5),
        'ssm_d': nrm((L, G, P), 1.0),
        'ssm_log_dt': jax.random.uniform(next(ks), (L, G), f32, math.log(1e-3), math.log(1e-1)),
        'ssm_glu_w': nrm((L, SSM_WIDTH, SSM_WIDTH), SSM_WIDTH ** -0.5),
        'ssm_glu_b': nrm((L, SSM_WIDTH), 0.01),
        'ssm_out_norm': gain((L, SSM_WIDTH)),
        'att_lambda_q1': nrm((L, ATT_QKDIM), 0.1),
        'att_lambda_k1': nrm((L, ATT_QKDIM), 0.1),
        'att_lambda_q2': nrm((L, ATT_QKDIM), 0.1),
        'att_lambda_k2': nrm((L, ATT_QKDIM), 0.1),
        'att_subln': gain((L, ATT_VDIM)),
        'dn_conv_w': nrm((L, DN_CONV, 3 * DN_WIDTH), DN_CONV ** -0.5),
        'dn_a_log': jnp.log(jax.random.uniform(next(ks), (L, DN_HEADS), f32, 1.0, 16.0)),
        'dn_dt_bias': dt + jnp.log(-jnp.expm1(-dt)),
        'dn_out_norm': gain((L, DN_HEAD_DIM)),
        'w_out': nrm((L, MIX_WIDTH, D), MIX_WIDTH ** -0.5),
        'norm_ffn2': gain((L, D)),
        'ffn2_w_gate': nrm((L, D, FFN_DIM), D ** -0.5),
        'ffn2_w_up': nrm((L, D, FFN_DIM), D ** -0.5),
        'ffn2_w_down': nrm((L, FFN_DIM, D), FFN_DIM ** -0.5),
        'ple_w': nrm((L, PLE_DIM, D), PLE_DIM ** -0.5),
        'ple_gate_norm': gain((L, D)),
        'ple_gate_w': nrm((L, D, D), D ** -0.5),
        'final_norm': gain((D,)),
    }


def reference(x_prompt, x_sample, cache_k, cache_v, state_ssm_re, state_ssm_im, state_conv, state_delta, page_table,
              p_prompt, p_sample, norm_ffn1, ffn1_w_gate, ffn1_w_up, ffn1_w_down, norm_mix, w_in,
              ssm_lambda_re, ssm_lambda_im, ssm_b_re, ssm_b_im, ssm_c_re, ssm_c_im, ssm_d, ssm_log_dt,
              ssm_glu_w, ssm_glu_b, ssm_out_norm, att_lambda_q1, att_lambda_k1, att_lambda_q2, att_lambda_k2,
              att_subln, dn_conv_w, dn_a_log, dn_dt_bias, dn_out_norm, w_out, norm_ffn2, ffn2_w_gate, ffn2_w_up,
              ffn2_w_down, ple_w, ple_gate_norm, ple_gate_w, final_norm):
    w = dict(norm_ffn1=norm_ffn1, ffn1_w_gate=ffn1_w_gate, ffn1_w_up=ffn1_w_up, ffn1_w_down=ffn1_w_down,
             norm_mix=norm_mix, w_in=w_in, ssm_lambda_re=ssm_lambda_re, ssm_lambda_im=ssm_lambda_im,
             ssm_b_re=ssm_b_re, ssm_b_im=ssm_b_im, ssm_c_re=ssm_c_re, ssm_c_im=ssm_c_im, ssm_d=ssm_d,
             ssm_log_dt=ssm_log_dt, ssm_glu_w=ssm_glu_w, ssm_glu_b=ssm_glu_b, ssm_out_norm=ssm_out_norm,
             att_lambda_q1=att_lambda_q1, att_lambda_k1=att_lambda_k1, att_lambda_q2=att_lambda_q2,
             att_lambda_k2=att_lambda_k2, att_subln=att_subln, dn_conv_w=dn_conv_w, dn_a_log=dn_a_log,
             dn_dt_bias=dn_dt_bias, dn_out_norm=dn_out_norm, w_out=w_out, norm_ffn2=norm_ffn2,
             ffn2_w_gate=ffn2_w_gate, ffn2_w_up=ffn2_w_up, ffn2_w_down=ffn2_w_down, ple_w=ple_w,
             ple_gate_norm=ple_gate_norm, ple_gate_w=ple_gate_w, final_norm=final_norm)
    y_prompt, (k_p, v_p, sre_p, sim_p, conv_p, dn_p) = trunk(x_prompt, p_prompt, 0, None, w)
    past = (cache_k, cache_v, page_table, state_ssm_re, state_ssm_im, state_conv, state_delta)
    y_sample, (k_s, v_s, sre_s, sim_s, conv_s, dn_s) = trunk(x_sample, p_sample, PAST_LEN, past, w)
    return (y_prompt, y_sample, k_p, v_p, sre_p, sim_p, conv_p, dn_p, k_s, v_s, sre_s, sim_s, conv_s, dn_s)
```

```python
import functools
import math

import jax
import jax.numpy as jnp
from jax import lax
from jax.experimental import pallas as pl
from jax.experimental.pallas import tpu as pltpu

F32 = jnp.float32
BF16 = jnp.bfloat16
NORM_EPS = 1e-6
NEG_BIG = -1e30
HIGHEST = lax.Precision.HIGHEST

SSM_GROUPS = 32
SSM_GROUP = 16
SSM_STATE = 64
SSM_WIDTH = SSM_GROUPS * SSM_GROUP
SSM_NS = SSM_GROUPS * SSM_STATE
ATT_HEADS = 8
ATT_VDIM = 128
ATT_QKDIM = 64
ATT_WIDTH = ATT_HEADS * ATT_VDIM
DN_HEADS = 4
DN_HEAD_DIM = 128
DN_WIDTH = DN_HEADS * DN_HEAD_DIM
DN_CONV = 4
DN_CHUNK = 64
PAGE_SIZE = 128

COL_SSM = 0
COL_Q = SSM_WIDTH
COL_K = COL_Q + ATT_WIDTH
COL_V = COL_K + ATT_WIDTH
COL_DN = COL_V + ATT_WIDTH
COL_Z = COL_DN + 3 * DN_WIDTH
COL_AB = COL_Z + DN_WIDTH
IN_COLS_PAD = COL_AB + 128

VMEM_LIMIT = 56 * 1024 * 1024


def _cparams(sem):
    return pltpu.CompilerParams(dimension_semantics=sem, vmem_limit_bytes=VMEM_LIMIT)


def _rms(x, g):
    return x * lax.rsqrt(jnp.mean(x * x, axis=-1, keepdims=True) + NORM_EPS) * g


def _bdot(a, b):
    return jnp.dot(a.astype(BF16), b.astype(BF16), preferred_element_type=F32)


def _bdot_nt(a, b):
    return lax.dot_general(a.astype(BF16), b.astype(BF16), (((1,), (1,)), ((), ())),
                           preferred_element_type=F32)


def _bdot_tn(a, b):
    return lax.dot_general(a.astype(BF16), b.astype(BF16), (((0,), (0,)), ((), ())),
                           preferred_element_type=F32)


def _hdot(a, b):
    return jnp.dot(a, b, precision=HIGHEST, preferred_element_type=F32)


def _ffn_body(x_ref, g_ref, wg_ref, wu_ref, wd_ref, o_ref, xn_ref, acc_ref):
    j = pl.program_id(1)

    @pl.when(j == 0)
    def _():
        xn_ref[...] = _rms(x_ref[...], g_ref[...]).astype(BF16)
        acc_ref[...] = jnp.zeros_like(acc_ref)

    xn = xn_ref[...]
    gate = jnp.dot(xn, wg_ref[...], preferred_element_type=F32)
    up = jnp.dot(xn, wu_ref[...], preferred_element_type=F32)
    a = gate * jax.nn.sigmoid(gate) * up
    acc_ref[...] += jnp.dot(a.astype(BF16), wd_ref[...], preferred_element_type=F32)

    @pl.when(j == pl.num_programs(1) - 1)
    def _():
        o_ref[...] = x_ref[...] + 0.5 * acc_ref[...]


def _ffn(x, g, wg, wu, wd, *, tm, tf):
    t, d = x.shape
    f = wg.shape[1]
    return pl.pallas_call(
        _ffn_body,
        grid=(pl.cdiv(t, tm), f // tf),
        in_specs=[
            pl.BlockSpec((tm, d), lambda i, j: (i, 0)),
            pl.BlockSpec((1, d), lambda i, j: (0, 0)),
            pl.BlockSpec((d, tf), lambda i, j: (0, j)),
            pl.BlockSpec((d, tf), lambda i, j: (0, j)),
            pl.BlockSpec((tf, d), lambda i, j: (j, 0)),
        ],
        out_specs=pl.BlockSpec((tm, d), lambda i, j: (i, 0)),
        out_shape=jax.ShapeDtypeStruct((t, d), F32),
        scratch_shapes=[pltpu.VMEM((tm, d), BF16), pltpu.VMEM((tm, d), F32)],
        compiler_params=_cparams(("parallel", "arbitrary")),
        name="ffn",
    )(x, g, wg, wu, wd)


def _norm_matmul_body(x_ref, g_ref, w_ref, o_ref, xn_ref):
    @pl.when(pl.program_id(1) == 0)
    def _():
        xn_ref[...] = _rms(x_ref[...], g_ref[...]).astype(BF16)

    o_ref[...] = jnp.dot(xn_ref[...], w_ref[...], preferred_element_type=F32)


def _norm_matmul(x, g, w, *, tm, tn):
    t, d = x.shape
    n = w.shape[1]
    return pl.pallas_call(
        _norm_matmul_body,
        grid=(pl.cdiv(t, tm), n // tn),
        in_specs=[
            pl.BlockSpec((tm, d), lambda i, j: (i, 0)),
            pl.BlockSpec((1, d), lambda i, j: (0, 0)),
            pl.BlockSpec((d, tn), lambda i, j: (0, j)),
        ],
        out_specs=pl.BlockSpec((tm, tn), lambda i, j: (i, j)),
        out_shape=jax.ShapeDtypeStruct((t, n), F32),
        scratch_shapes=[pltpu.VMEM((tm, d), BF16)],
        compiler_params=_cparams(("parallel", "arbitrary")),
        name="in_proj",
    )(x, g, w)


def _out_proj_body(h_ref, a_ref, b_ref, c_ref, w_ref, o_ref):
    acc = h_ref[...]
    acc += _bdot(a_ref[...], w_ref[0:SSM_WIDTH, :])
    acc += _bdot(b_ref[...], w_ref[SSM_WIDTH:SSM_WIDTH + ATT_WIDTH, :])
    acc += _bdot(c_ref[...], w_ref[SSM_WIDTH + ATT_WIDTH:, :])
    o_ref[...] = acc


def _out_proj(h, a, b, c, w, *, tm):
    t, d = h.shape
    row = lambda width: pl.BlockSpec((tm, width), lambda i: (i, 0))
    return pl.pallas_call(
        _out_proj_body,
        grid=(pl.cdiv(t, tm),),
        in_specs=[row(d), row(a.shape[1]), row(b.shape[1]), row(c.shape[1]),
                  pl.BlockSpec(w.shape, lambda i: (0, 0))],
        out_specs=row(d),
        out_shape=jax.ShapeDtypeStruct((t, d), F32),
        compiler_params=_cparams(("parallel",)),
        name="out_proj",
    )(h, a, b, c, w)


def _ple_body(h_ref, g_ref, wg_ref, p_ref, wp_ref, fg_ref, o_ref, *, final):
    h = h_ref[...]
    gate = jax.nn.sigmoid(_bdot(_rms(h, g_ref[...]), wg_ref[...]))
    y = h + gate * _bdot(p_ref[...], wp_ref[...])
    if final:
        y = _rms(y, fg_ref[...])
    o_ref[...] = y


def _ple(h, g, wg, p, wp, fg, *, tm, final):
    t, d = h.shape
    pd = p.shape[1]
    const = lambda shape: pl.BlockSpec(shape, lambda i: (0, 0))
    return pl.pallas_call(
        functools.partial(_ple_body, final=final),
        grid=(pl.cdiv(t, tm),),
        in_specs=[pl.BlockSpec((tm, d), lambda i: (i, 0)), const((1, d)), const((d, d)),
                  pl.BlockSpec((tm, pd), lambda i: (i, 0)), const((pd, d)), const((1, d))],
        out_specs=pl.BlockSpec((tm, d), lambda i: (i, 0)),
        out_shape=jax.ShapeDtypeStruct((t, d), F32),
        compiler_params=_cparams(("parallel",)),
        name="ple",
    )(h, g, wg, p, wp, fg)


SSM_Q = 8
SSM_LANES = 512


def _gelu_tanh(x):
    return 0.5 * x * (1.0 + jnp.tanh(math.sqrt(2.0 / math.pi) * (x + 0.044715 * (x * x * x))))


def _s5_body(u_ref, s0_ref, lam_ref, pw_ref, b_ref, c_ref, d_ref, glw_ref, glb_ref, on_ref,
             o_ref, sfin_ref, up_ref, st_ref, end_ref, carry_ref, sin_ref, yp_ref, pln_ref, *, L, chained):
    ns = SSM_NS
    ci = pl.program_id(1)

    for j in range(SSM_WIDTH // 128):
        pln_ref[j] = u_ref[:, j * 128:(j + 1) * 128]
    for j in range(SSM_WIDTH // 128):
        for r in range(L):
            up_ref[r * SSM_Q:(r + 1) * SSM_Q, j * 128:(j + 1) * 128] = pln_ref.at[j][pl.ds(r, SSM_Q, stride=L), :]
    up = up_ref[...]
    st_ref[...] = _bdot(up, b_ref[...])

    for sl in range(ns // SSM_LANES):
        cre = slice(sl * SSM_LANES, (sl + 1) * SSM_LANES)
        cim = slice(ns + sl * SSM_LANES, ns + (sl + 1) * SSM_LANES)
        lr = jnp.broadcast_to(lam_ref[0:1, cre], (SSM_Q, SSM_LANES))
        li = jnp.broadcast_to(lam_ref[1:2, cre], (SSM_Q, SSM_LANES))
        if chained:
            init = (jnp.zeros((SSM_Q, SSM_LANES), F32), jnp.zeros((SSM_Q, SSM_LANES), F32))
        else:
            init = (s0_ref[:, cre], s0_ref[:, cim])

        def step(r, st, cre=cre, cim=cim, lr=lr, li=li):
            sr, si = st
            row = pl.multiple_of(r * SSM_Q, SSM_Q)
            nr = lr * sr - li * si + st_ref[pl.ds(row, SSM_Q), cre]
            ni = lr * si + li * sr + st_ref[pl.ds(row, SSM_Q), cim]
            st_ref[pl.ds(row, SSM_Q), cre] = nr
            st_ref[pl.ds(row, SSM_Q), cim] = ni
            return nr, ni

        sr, si = lax.fori_loop(0, L, step, init)
        end_ref[:, cre] = sr
        end_ref[:, cim] = si

    if chained:
        @pl.when(ci == 0)
        def _():
            carry_ref[...] = s0_ref[...]

        plr = pw_ref[L - 1:L, 0:ns]
        pli = pw_ref[L - 1:L, ns:]
        cr = carry_ref[:, 0:ns]
        cj = carry_ref[:, ns:]
        for q in range(SSM_Q):
            sin_ref[q:q + 1, 0:ns] = cr
            sin_ref[q:q + 1, ns:] = cj
            er = end_ref[q:q + 1, 0:ns]
            ei = end_ref[q:q + 1, ns:]
            cr, cj = er + plr * cr - pli * cj, ei + plr * cj + pli * cr
        carry_ref[:, 0:ns] = cr
        carry_ref[:, ns:] = cj

        for sl in range(ns // SSM_LANES):
            cre = slice(sl * SSM_LANES, (sl + 1) * SSM_LANES)
            cim = slice(ns + sl * SSM_LANES, ns + (sl + 1) * SSM_LANES)
            inr = sin_ref[:, cre]
            ini = sin_ref[:, cim]

            def fix(r, carry, cre=cre, cim=cim, inr=inr, ini=ini):
                row = pl.multiple_of(r * SSM_Q, SSM_Q)
                pr = jnp.broadcast_to(pw_ref[pl.ds(r, 1), cre], (SSM_Q, SSM_LANES))
                pi = jnp.broadcast_to(pw_ref[pl.ds(r, 1), cim], (SSM_Q, SSM_LANES))
                st_ref[pl.ds(row, SSM_Q), cre] += pr * inr - pi * ini
                st_ref[pl.ds(row, SSM_Q), cim] += pr * ini + pi * inr
                return carry

            lax.fori_loop(0, L, fix, 0)

        @pl.when(ci == pl.num_programs(1) - 1)
        def _():
            sfin_ref[...] = carry_ref[...]
    else:
        sfin_ref[...] = end_ref[...]

    y = _bdot(st_ref[...], c_ref[...]) + d_ref[...] * up
    y = _gelu_tanh(y)
    y = y * jax.nn.sigmoid(_bdot(y, glw_ref[...]) + glb_ref[...])
    yp_ref[...] = _rms(y, on_ref[...])
    for j in range(SSM_WIDTH // 128):
        for r in range(L):
            pln_ref.at[j][pl.ds(r, SSM_Q, stride=L), :] = yp_ref[r * SSM_Q:(r + 1) * SSM_Q, j * 128:(j + 1) * 128]
    for j in range(SSM_WIDTH // 128):
        o_ref[:, j * 128:(j + 1) * 128] = pln_ref[j]


def _s5(proj, s0, prm, *, row_block0, nb, n_chunks, L, chained):
    rows = SSM_Q * L
    ns2 = 2 * SSM_NS
    q0 = s0.shape[1]
    const = lambda a: pl.BlockSpec(a.shape, lambda b, c: (0,) * a.ndim)
    lam, pw, bblk, cblk, dvec, glw, glb, on = prm
    out, sfin = pl.pallas_call(
        functools.partial(_s5_body, L=L, chained=chained),
        grid=(nb, n_chunks),
        in_specs=[
            pl.BlockSpec((rows, SSM_WIDTH), lambda b, c: (row_block0 + b * n_chunks + c, COL_SSM // SSM_WIDTH)),
            pl.BlockSpec((None, q0, ns2), lambda b, c: (b, 0, 0)),
            const(lam), const(pw), const(bblk), const(cblk), const(dvec), const(glw), const(glb), const(on),
        ],
        out_specs=[
            pl.BlockSpec((rows, SSM_WIDTH), lambda b, c: (b * n_chunks + c, 0)),
            pl.BlockSpec((None, q0, ns2), lambda b, c: (b, 0, 0)),
        ],
        out_shape=[
            jax.ShapeDtypeStruct((nb * n_chunks * rows, SSM_WIDTH), F32),
            jax.ShapeDtypeStruct((nb, q0, ns2), F32),
        ],
        scratch_shapes=[
            pltpu.VMEM((rows, SSM_WIDTH), F32),
            pltpu.VMEM((rows, ns2), F32),
            pltpu.VMEM((SSM_Q, ns2), F32),
            pltpu.VMEM((1, ns2), F32),
            pltpu.VMEM((SSM_Q, ns2), F32),
            pltpu.VMEM((rows, SSM_WIDTH), F32),
            pltpu.VMEM((SSM_WIDTH // 128, rows, 128), F32),
        ],
        compiler_params=_cparams(("parallel", "arbitrary")),
        name="s5_chained" if chained else "s5_batch",
    )(proj, s0, lam, pw, bblk, cblk, dvec, glw, glb, on)
    return out, sfin


def _s5_params(lre, lim, bre, bim, cre, cim, dd, log_dt, glw, glb, on, *, n_pow):
    g, n = lre.shape
    dt = jnp.exp(log_dt)[:, None]

    def cexp(k):
        mag = jnp.exp(lre * dt * k)
        return mag * jnp.cos(lim * dt * k), mag * jnp.sin(lim * dt * k)

    ar, ai = cexp(1.0)
    den = lre * lre + lim * lim
    fr = ((ar - 1.0) * lre + ai * lim) / den
    fi = (ai * lre - (ar - 1.0) * lim) / den
    bbr = fr[..., None] * bre - fi[..., None] * bim
    bbi = fr[..., None] * bim + fi[..., None] * bre
    eye = jnp.eye(g, dtype=F32)
    blk_b = lambda a: jnp.einsum('gnp,gh->gphn', a, eye).reshape(g * SSM_GROUP, g * n)
    blk_c = lambda a: jnp.einsum('gpn,gh->gnhp', a, eye).reshape(g * n, g * SSM_GROUP)
    bblk = jnp.concatenate([blk_b(bbr), blk_b(bbi)], axis=1).astype(BF16)
    cblk = jnp.concatenate([blk_c(cre), -blk_c(cim)], axis=0).astype(BF16)
    lam = jnp.stack([ar.reshape(-1), ai.reshape(-1)])
    ks = jnp.arange(1, n_pow + 1, dtype=F32)[:, None, None]
    pr, pi = cexp(ks)
    pw = jnp.concatenate([pr.reshape(n_pow, -1), pi.reshape(n_pow, -1)], axis=1)
    return (lam, pw, bblk, cblk, dd.reshape(1, -1), glw.astype(BF16), glb.reshape(1, -1), on.reshape(1, -1))


def _att_lambda(lq1_ref, lk1_ref, lq2_ref, lk2_ref, lam_init):
    a = jnp.sum(lq1_ref[...] * lk1_ref[...], axis=-1, keepdims=True)
    b = jnp.sum(lq2_ref[...] * lk2_ref[...], axis=-1, keepdims=True)
    return jnp.exp(a) - jnp.exp(b) + lam_init


def _softmax_update(s, v, m_ref, l_ref, acc_ref, idx):
    m_prev = m_ref[idx][:, 0:1]
    l_prev = l_ref[idx][:, 0:1]
    m_new = jnp.maximum(m_prev, jnp.max(s, axis=-1, keepdims=True))
    alpha = jnp.exp(m_prev - m_new)
    p = jnp.exp(s - m_new)
    l_new = alpha * l_prev + jnp.sum(p, axis=-1, keepdims=True)
    acc_ref[idx] = alpha * acc_ref[idx] + _bdot(p, v)
    m_ref[idx] = jnp.broadcast_to(m_new, m_ref.shape[1:])
    l_ref[idx] = jnp.broadcast_to(l_new, l_ref.shape[1:])


def _attn_prompt_body(qi_tab, ki_tab, slopes, q_ref, k_ref, v_ref, lq1_ref, lk1_ref, lq2_ref, lk2_ref,
                      sub_ref, o_ref, m_ref, l_ref, acc_ref, *, tq, lam_init):
    h = pl.program_id(1)
    t = pl.program_id(2)
    qi = qi_tab[t]
    ki = ki_tab[t]

    @pl.when(ki == 0)
    def _():
        m_ref[...] = jnp.full_like(m_ref, NEG_BIG)
        l_ref[...] = jnp.zeros_like(l_ref)
        acc_ref[...] = jnp.zeros_like(acc_ref)

    q = (q_ref[...] * (ATT_QKDIM ** -0.5)).astype(BF16)
    k = k_ref[...].astype(BF16)
    v = v_ref[...].astype(BF16)
    kpos = ki * tq + lax.broadcasted_iota(jnp.int32, (1, tq), 1)
    qpos = qi * tq + lax.broadcasted_iota(jnp.int32, (tq, 1), 0)
    bias = slopes[h] * kpos.astype(F32)
    causal = kpos <= qpos
    for c in range(2):
        cs = slice(c * ATT_QKDIM, (c + 1) * ATT_QKDIM)
        s = _bdot_nt(q[:, cs], k[:, cs]) + bias
        s = jnp.where(causal, s, NEG_BIG)
        _softmax_update(s, v, m_ref, l_ref, acc_ref, c)

    @pl.when(ki == qi)
    def _():
        lam = _att_lambda(lq1_ref, lk1_ref, lq2_ref, lk2_ref, lam_init)
        o = acc_ref[0] / l_ref[0][:, 0:1] - lam * (acc_ref[1] / l_ref[1][:, 0:1])
        o_ref[...] = _rms(o, sub_ref[...]) * (1.0 - lam_init)


def _attn_prompt(proj, lam_vecs, sub, *, nb, seq, tq, lam_init):
    nq = seq // tq
    pairs = [(a, b) for a in range(nq) for b in range(a + 1)]
    qi_tab = jnp.asarray([p[0] for p in pairs], jnp.int32)
    ki_tab = jnp.asarray([p[1] for p in pairs], jnp.int32)
    slopes = jnp.exp2(-8.0 * jnp.arange(1, ATT_HEADS + 1, dtype=F32) / ATT_HEADS)
    vec = pl.BlockSpec((1, ATT_QKDIM), lambda b, h, t, *_: (0, 0))
    grid_spec = pltpu.PrefetchScalarGridSpec(
        num_scalar_prefetch=3,
        grid=(nb, ATT_HEADS, len(pairs)),
        in_specs=[
            pl.BlockSpec((tq, ATT_VDIM), lambda b, h, t, qt, kt, sl: (b * nq + qt[t], COL_Q // ATT_VDIM + h)),
            pl.BlockSpec((tq, ATT_VDIM), lambda b, h, t, qt, kt, sl: (b * nq + kt[t], COL_K // ATT_VDIM + h)),
            pl.BlockSpec((tq, ATT_VDIM), lambda b, h, t, qt, kt, sl: (b * nq + kt[t], COL_V // ATT_VDIM + h)),
            vec, vec, vec, vec,
            pl.BlockSpec((1, ATT_VDIM), lambda b, h, t, *_: (0, 0)),
        ],
        out_specs=pl.BlockSpec((tq, ATT_VDIM), lambda b, h, t, qt, kt, sl: (b * nq + qt[t], h)),
        scratch_shapes=[pltpu.VMEM((2, tq, 128), F32), pltpu.VMEM((2, tq, 128), F32),
                        pltpu.VMEM((2, tq, ATT_VDIM), F32)],
    )
    return pl.pallas_call(
        functools.partial(_attn_prompt_body, tq=tq, lam_init=lam_init),
        grid_spec=grid_spec,
        out_shape=jax.ShapeDtypeStruct((nb * seq, ATT_WIDTH), F32),
        compiler_params=_cparams(("parallel", "parallel", "arbitrary")),
        name="attn_prompt",
    )(qi_tab, ki_tab, slopes, proj, proj, proj, *lam_vecs, sub)


ATT_ROWS = ATT_HEADS * 2 * 8


def _attn_sample_body(pt_ref, q0_ref, q1_ref, k0_ref, k1_ref, v0_ref, v1_ref, kc_ref, vc_ref,
                      lq1_ref, lk1_ref, lq2_ref, lk2_ref, sub_ref, o_ref,
                      qbd_ref, m_ref, l_ref, acc_ref, *, past_len, tnew, lam_init):
    p = pl.program_id(1)
    rows = lax.broadcasted_iota(jnp.int32, (ATT_ROWS, 1), 0)
    qrow = rows & (tnew - 1)
    slope = jnp.exp2(-((rows >> 4) + 1).astype(F32) * (8.0 / ATT_HEADS))
    cols = lax.broadcasted_iota(jnp.int32, (1, PAGE_SIZE), 1)

    @pl.when(p == 0)
    def _():
        q = jnp.concatenate([q0_ref[...], q1_ref[...]], axis=-1) * (ATT_QKDIM ** -0.5)
        qt = jnp.concatenate([q] * (ATT_ROWS // tnew), axis=0)
        colblk = lax.broadcasted_iota(jnp.int32, (1, ATT_WIDTH), 1) >> 6
        qbd_ref[...] = jnp.where((rows >> 3) == colblk, qt, 0.0).astype(BF16)
        m_ref[...] = jnp.full_like(m_ref, NEG_BIG)
        l_ref[...] = jnp.zeros_like(l_ref)
        acc_ref[...] = jnp.zeros_like(acc_ref)

    dist = (past_len + qrow - (p * PAGE_SIZE + cols)).astype(F32)
    s = _bdot_nt(qbd_ref[...], kc_ref[...]) - slope * dist
    _softmax_update(s, vc_ref[...], m_ref, l_ref, acc_ref, 0)

    @pl.when(p == pl.num_programs(1) - 1)
    def _():
        pad = jnp.zeros((PAGE_SIZE - tnew, ATT_WIDTH), F32)
        kn = jnp.concatenate([k0_ref[...], k1_ref[...], ], axis=-1)
        vn = jnp.concatenate([v0_ref[...], v1_ref[...], ], axis=-1)
        kn = jnp.concatenate([kn, pad], axis=0)
        vn = jnp.concatenate([vn, pad], axis=0)
        s = _bdot_nt(qbd_ref[...], kn) - slope * (qrow - cols).astype(F32)
        s = jnp.where(cols <= qrow, s, NEG_BIG)
        _softmax_update(s, vn, m_ref, l_ref, acc_ref, 0)

        lam = _att_lambda(lq1_ref, lk1_ref, lq2_ref, lk2_ref, lam_init)
        on = acc_ref[0] / l_ref[0][:, 0:1]
        for h in range(ATT_HEADS):
            cs = slice(h * ATT_VDIM, (h + 1) * ATT_VDIM)
            o = on[h * 16:h * 16 + tnew, cs] - lam * on[h * 16 + 8:h * 16 + 8 + tnew, cs]
            o_ref[:, cs] = _rms(o, sub_ref[...]) * (1.0 - lam_init)


def _attn_sample(proj, cache_k, cache_v, page_table, lam_vecs, sub, *, layer, row_block0, nb, tnew,
                 past_len, lam_init):
    assert tnew == 8
    n_pages = page_table.shape[1]
    half = ATT_WIDTH // 2
    new = lambda col: pl.BlockSpec((tnew, half), lambda b, p, pt: (row_block0 + b, col // half))
    vec = pl.BlockSpec((1, ATT_QKDIM), lambda b, p, pt: (0, 0))
    page = pl.BlockSpec((None, None, PAGE_SIZE, ATT_WIDTH), lambda b, p, pt: (layer, pt[b, p], 0, 0))
    grid_spec = pltpu.PrefetchScalarGridSpec(
        num_scalar_prefetch=1,
        grid=(nb, n_pages),
        in_specs=[new(COL_Q), new(COL_Q + half), new(COL_K), new(COL_K + half), new(COL_V), new(COL_V + half),
                  page, page, vec, vec, vec, vec,
                  pl.BlockSpec((1, ATT_VDIM), lambda b, p, pt: (0, 0))],
        out_specs=pl.BlockSpec((tnew, ATT_WIDTH), lambda b, p, pt: (b, 0)),
        scratch_shapes=[pltpu.VMEM((ATT_ROWS, ATT_WIDTH), BF16), pltpu.VMEM((1, ATT_ROWS, 128), F32),
                        pltpu.VMEM((1, ATT_ROWS, 128), F32), pltpu.VMEM((1, ATT_ROWS, ATT_WIDTH), F32)],
    )
    return pl.pallas_call(
        functools.partial(_attn_sample_body, past_len=past_len, tnew=tnew, lam_init=lam_init),
        grid_spec=grid_spec,
        out_shape=jax.ShapeDtypeStruct((nb * tnew, ATT_WIDTH), F32),
        compiler_params=_cparams(("parallel", "arbitrary")),
        name="attn_sample",
    )(page_table, proj, proj, proj, proj, proj, proj, cache_k, cache_v, *lam_vecs, sub)


def _unit_lower_inverse(n_mat, eye):
    inv = eye - n_mat
    pw = n_mat
    for _ in range(int(math.log2(DN_CHUNK)) - 1):
        pw = _hdot(pw, pw)
        inv = inv + _hdot(inv, pw)
    return inv


def _l2norm(x):
    return x * lax.rsqrt(jnp.sum(x * x, axis=-1, keepdims=True) + NORM_EPS)


def _delta_body(xq_ref, xk_ref, xv_ref, z_ref, ab_ref, conv0_ref, s0_ref, cw_ref, alog_ref, dtb_ref, on_ref,
                o_ref, convn_ref, sfin_ref, xx_ref, s_ref, *, t):
    c = DN_CHUNK
    w3 = 3 * DN_WIDTH
    ci = pl.program_id(1)

    @pl.when(ci == 0)
    def _():
        xx_ref[0:8, :] = conv0_ref[...]
        s_ref[...] = s0_ref[...]

    xx_ref[8:8 + t, 0:DN_WIDTH] = xq_ref[...]
    xx_ref[8:8 + t, DN_WIDTH:2 * DN_WIDTH] = xk_ref[...]
    xx_ref[8:8 + t, 2 * DN_WIDTH:w3] = xv_ref[...]
    y = jnp.zeros((t, w3), F32)
    for i in range(DN_CONV):
        y = y + cw_ref[i:i + 1, :] * xx_ref[pl.ds(8 - (DN_CONV - 1) + i, t), :]
    tail = xx_ref[t:t + 8, :]
    convn_ref[...] = tail
    xx_ref[0:8, :] = tail

    y = y * jax.nn.sigmoid(y)
    ab = ab_ref[...]
    g_all = -jnp.exp(alog_ref[...]) * jax.nn.softplus(ab + dtb_ref[...])
    beta_all = jax.nn.sigmoid(ab)
    if t < c:
        zpad = lambda a: jnp.concatenate([a, jnp.zeros((c - t, a.shape[1]), F32)], axis=0)
    else:
        zpad = lambda a: a
    g_all = zpad(g_all)
    beta_all = zpad(beta_all)

    row = lax.broadcasted_iota(jnp.int32, (c, c), 0)
    col = lax.broadcasted_iota(jnp.int32, (c, c), 1)
    eye = (row == col).astype(F32)
    tril = (row >= col).astype(F32)
    ones = jnp.ones((c, c), F32)
    gc = _hdot(tril, g_all)
    eg = jnp.exp(gc)
    z = z_ref[...]

    for h in range(DN_HEADS):
        hs = slice(h * DN_HEAD_DIM, (h + 1) * DN_HEAD_DIM)
        q = zpad(_l2norm(y[:, hs]) * (DN_HEAD_DIM ** -0.5))
        k = zpad(_l2norm(y[:, DN_WIDTH + h * DN_HEAD_DIM:DN_WIDTH + (h + 1) * DN_HEAD_DIM]))
        v = zpad(y[:, 2 * DN_WIDTH + h * DN_HEAD_DIM:2 * DN_WIDTH + (h + 1) * DN_HEAD_DIM])
        beta = beta_all[:, DN_HEADS + h:DN_HEADS + h + 1]
        gch = gc[:, h:h + 1]
        egh = eg[:, h:h + 1]
        glast = gc[c - 1:c, h:h + 1]
        gc_cols = _hdot(ones, eye * gch)
        decay = jnp.exp(jnp.where(row >= col, gch - gc_cols, NEG_BIG))
        kb = k * beta
        n_mat = jnp.where(row > col, _bdot_nt(kb, k) * decay, 0.0)
        inv = _unit_lower_inverse(n_mat, eye)
        sol = _hdot(inv, jnp.concatenate([v * beta, kb * egh], axis=-1))
        u = sol[:, 0:DN_HEAD_DIM]
        w = sol[:, DN_HEAD_DIM:]
        attn = _bdot_nt(q, k) * decay
        qg = q * egh
        kd = k * jnp.exp(glast - gch)
        s = s_ref[h]
        v_new = u - _bdot(w, s)
        o = _bdot(qg, s) + _bdot(attn, v_new)
        s_ref[h] = s * jnp.exp(glast) + _bdot_tn(kd, v_new)
        zh = z[:, hs]
        o_ref[:, hs] = _rms(o[0:t], on_ref[...]) * (zh * jax.nn.sigmoid(zh))

    @pl.when(ci == pl.num_programs(1) - 1)
    def _():
        sfin_ref[...] = s_ref[...]


def _delta(proj, conv0, s0, cw, alog, dtb, on, *, row_block0, nb, n_chunks, t):
    w3 = 3 * DN_WIDTH
    blk = lambda col, width: pl.BlockSpec((t, width), lambda b, c: (row_block0 + b * n_chunks + c, col // width))
    const = lambda a: pl.BlockSpec(a.shape, lambda b, c: (0,) * a.ndim)
    out, convn, sfin = pl.pallas_call(
        functools.partial(_delta_body, t=t),
        grid=(nb, n_chunks),
        in_specs=[
            blk(COL_DN, DN_WIDTH), blk(COL_DN + DN_WIDTH, DN_WIDTH), blk(COL_DN + 2 * DN_WIDTH, DN_WIDTH),
            blk(COL_Z, DN_WIDTH), blk(COL_AB, 128),
            pl.BlockSpec((None, 8, w3), lambda b, c: (b, 0, 0)),
            pl.BlockSpec((None, DN_HEADS, DN_HEAD_DIM, DN_HEAD_DIM), lambda b, c: (b, 0, 0, 0)),
            const(cw), const(alog), const(dtb), const(on),
        ],
        out_specs=[
            pl.BlockSpec((t, DN_WIDTH), lambda b, c: (b * n_chunks + c, 0)),
            pl.BlockSpec((None, 8, w3), lambda b, c: (b, 0, 0)),
            pl.BlockSpec((None, DN_HEADS, DN_HEAD_DIM, DN_HEAD_DIM), lambda b, c: (b, 0, 0, 0)),
        ],
        out_shape=[
            jax.ShapeDtypeStruct((nb * n_chunks * t, DN_WIDTH), F32),
            jax.ShapeDtypeStruct((nb, 8, w3), F32),
            jax.ShapeDtypeStruct((nb, DN_HEADS, DN_HEAD_DIM, DN_HEAD_DIM), F32),
        ],
        scratch_shapes=[pltpu.VMEM((t + 8, w3), F32), pltpu.VMEM((DN_HEADS, DN_HEAD_DIM, DN_HEAD_DIM), F32)],
        compiler_params=_cparams(("parallel", "arbitrary")),
        name="delta",
    )(proj, proj, proj, proj, proj, conv0, s0, cw, alog, dtb, on)
    return out, convn, sfin


TOKEN_BLOCK = 688
FFN_BLOCK = 512
PROJ_BLOCK = 1152
SSM_RUN = 64
ATT_BLOCK = 512


def kernel(x_prompt, x_sample, cache_k, cache_v, state_ssm_re, state_ssm_im, state_conv, state_delta, page_table, p_prompt, p_sample, norm_ffn1, ffn1_w_gate, ffn1_w_up, ffn1_w_down, norm_mix, w_in, ssm_lambda_re, ssm_lambda_im, ssm_b_re, ssm_b_im, ssm_c_re, ssm_c_im, ssm_d, ssm_log_dt, ssm_glu_w, ssm_glu_b, ssm_out_norm, att_lambda_q1, att_lambda_k1, att_lambda_q2, att_lambda_k2, att_subln, dn_conv_w, dn_a_log, dn_dt_bias, dn_out_norm, w_out, norm_ffn2, ffn2_w_gate, ffn2_w_up, ffn2_w_down, ple_w, ple_gate_norm, ple_gate_w, final_norm):
    nbp, seq, d = x_prompt.shape
    nbs, tnew, _ = x_sample.shape
    depth = w_in.shape[0]
    tp = nbp * seq
    ts = nbs * tnew
    past_len = page_table.shape[1] * PAGE_SIZE
    n_phys = cache_k.shape[1]

    x = jnp.concatenate([x_prompt.reshape(tp, d), x_sample.reshape(ts, d)], axis=0)
    ple_in = jnp.concatenate([p_prompt.reshape(depth, tp, -1), p_sample.reshape(depth, ts, -1)], axis=1)
    cache_k2 = cache_k.reshape(depth, n_phys, PAGE_SIZE, ATT_WIDTH)
    cache_v2 = cache_v.reshape(depth, n_phys, PAGE_SIZE, ATT_WIDTH)
    row = lambda a: a.reshape(1, -1)
    lane_pad = lambda a: jnp.pad(a.reshape(1, -1), ((0, 0), (0, 128 - a.size)))

    n_chunks_ssm = seq // (SSM_Q * SSM_RUN)
    outs = []
    for i in range(depth):
        lam_init = 0.8 - 0.6 * math.exp(-0.3 * i)
        h = _ffn(x, row(norm_ffn1[i]), ffn1_w_gate[i].astype(BF16), ffn1_w_up[i].astype(BF16),
                 ffn1_w_down[i].astype(BF16), tm=TOKEN_BLOCK, tf=FFN_BLOCK)
        w_in_pad = jnp.pad(w_in[i], ((0, 0), (0, IN_COLS_PAD - w_in.shape[2]))).astype(BF16)
        proj = _norm_matmul(h, row(norm_mix[i]), w_in_pad, tm=TOKEN_BLOCK, tn=PROJ_BLOCK)

        s5_args = (ssm_lambda_re[i], ssm_lambda_im[i], ssm_b_re[i], ssm_b_im[i], ssm_c_re[i], ssm_c_im[i],
                   ssm_d[i], ssm_log_dt[i], ssm_glu_w[i], ssm_glu_b[i], ssm_out_norm[i])
        prm_p = _s5_params(*s5_args, n_pow=SSM_RUN)
        prm_s = _s5_params(*s5_args, n_pow=tnew)
        ssm_p, sfin_p = _s5(proj, jnp.zeros((nbp, 1, 2 * SSM_NS), F32), prm_p, row_block0=0, nb=nbp,
                            n_chunks=n_chunks_ssm, L=SSM_RUN, chained=True)
        s0_s = jnp.concatenate([state_ssm_re[i].reshape(1, nbs, SSM_NS), state_ssm_im[i].reshape(1, nbs, SSM_NS)],
                               axis=-1)
        ssm_s, sfin_s = _s5(proj, s0_s, prm_s, row_block0=tp // ts, nb=1, n_chunks=1, L=tnew, chained=False)

        lam_vecs = (row(att_lambda_q1[i]), row(att_lambda_k1[i]), row(att_lambda_q2[i]), row(att_lambda_k2[i]))
        sub = row(att_subln[i])
        att_p = _attn_prompt(proj, lam_vecs, sub, nb=nbp, seq=seq, tq=ATT_BLOCK, lam_init=lam_init)
        att_s = _attn_sample(proj, cache_k2, cache_v2, page_table, lam_vecs, sub, layer=i,
                             row_block0=tp // tnew, nb=nbs, tnew=tnew, past_len=past_len, lam_init=lam_init)

        dn_args = (dn_conv_w[i], lane_pad(dn_a_log[i]), lane_pad(dn_dt_bias[i]), row(dn_out_norm[i]))
        conv0_p = jnp.zeros((nbp, 8, 3 * DN_WIDTH), F32)
        s0_p = jnp.zeros((nbp, DN_HEADS, DN_HEAD_DIM, DN_HEAD_DIM), F32)
        dn_p, conv_p, dfin_p = _delta(proj, conv0_p, s0_p, *dn_args, row_block0=0, nb=nbp,
                                      n_chunks=seq // DN_CHUNK, t=DN_CHUNK)
        conv0_s = jnp.pad(state_conv[i], ((0, 0), (8 - (DN_CONV - 1), 0), (0, 0)))
        dn_s, conv_s, dfin_s = _delta(proj, conv0_s, state_delta[i], *dn_args, row_block0=tp // tnew, nb=nbs,
                                      n_chunks=1, t=tnew)

        cat = lambda a, b: jnp.concatenate([a, b], axis=0)
        h = _out_proj(h, cat(ssm_p, ssm_s), cat(att_p, att_s), cat(dn_p, dn_s), w_out[i].astype(BF16),
                      tm=TOKEN_BLOCK)
        h = _ffn(h, row(norm_ffn2[i]), ffn2_w_gate[i].astype(BF16), ffn2_w_up[i].astype(BF16),
                 ffn2_w_down[i].astype(BF16), tm=TOKEN_BLOCK, tf=FFN_BLOCK)
        x = _ple(h, row(ple_gate_norm[i]), ple_gate_w[i].astype(BF16), ple_in[i], ple_w[i].astype(BF16),
                 row(final_norm), tm=TOKEN_BLOCK, final=(i == depth - 1))

        kv = lambda col, r0, nb_, t_: proj[r0:r0 + nb_ * t_, col:col + ATT_WIDTH].reshape(nb_, t_, ATT_HEADS, ATT_VDIM)
        ssm_state = lambda s, nb_: (s[..., :SSM_NS].reshape(nb_, SSM_GROUPS, SSM_STATE),
                                    s[..., SSM_NS:].reshape(nb_, SSM_GROUPS, SSM_STATE))
        outs.append((
            kv(COL_K, 0, nbp, seq), kv(COL_V, 0, nbp, seq), *ssm_state(sfin_p, nbp),
            conv_p[:, 8 - (DN_CONV - 1):], dfin_p,
            kv(COL_K, tp, nbs, tnew), kv(COL_V, tp, nbs, tnew), *ssm_state(sfin_s, nbs),
            conv_s[:, 8 - (DN_CONV - 1):], dfin_s,
        ))

    stacked = [jnp.stack(a) for a in zip(*outs)]
    y_prompt = x[:tp].reshape(nbp, seq, d)
    y_sample = x[tp:].reshape(nbs, tnew, d)
    return (y_prompt, y_sample, *stacked)
```
